```python
import math
import jax, jax.numpy as jnp
from jax import lax
import numpy as np

D_MODEL = 1024
BATCH = 4
SEQ = 4096
DEPTH = 4

N_MIXERS = 3
N_A = (DEPTH + 2) // 3
N_B = (DEPTH + 1) // 3
N_C = DEPTH // 3

PLE_DIM = 256
D_FF = 4 * D_MODEL
DEEP_ALPHA = (2.0 * DEPTH) ** 0.25
DEEP_BETA = (8.0 * DEPTH) ** -0.25
LN_EPS = 1e-5

REL_BUCKETS = 32
REL_MAX_DIST = 128

DA_HEADS = D_MODEL // 128
DA_HEAD_DIM = 64
DA_V_DIM = 2 * DA_HEAD_DIM
DA_EPS = 1e-5
Q_BLOCK = 128

GLA_HEADS = 4
GLA_DK = D_MODEL // 2 // GLA_HEADS
GLA_DV = D_MODEL // GLA_HEADS
GLA_GATE_RANK = 16
GLA_TAU = 16.0
GLA_CHUNK = 64
GLA_EPS = 1e-5

RW_HEAD = 64
RW_HEADS = D_MODEL // RW_HEAD
RW_DECAY_LORA = 64
RW_A_LORA = 64
RW_GATE_LORA = 128
RW_GN_EPS = 64e-5

kernel_name = 'hybrid_diffattn_gla_rwkv7_trunk'


def layer_norm(x, g, b, eps):
    xf = x.astype(jnp.float32)
    mu = jnp.mean(xf, -1, keepdims=True)
    var = jnp.mean(jnp.square(xf - mu), -1, keepdims=True)
    return ((xf - mu) * lax.rsqrt(var + eps)).astype(x.dtype) * g + b


def rms_norm(x, g, eps):
    xf = x.astype(jnp.float32)
    return (xf * lax.rsqrt(jnp.mean(xf * xf, -1, keepdims=True) + eps)).astype(x.dtype) * g


def t5_bucket(rel):
    n = jnp.maximum(rel, 0)
    max_exact = REL_BUCKETS // 2
    nf = jnp.maximum(n, 1).astype(jnp.float32)
    large = max_exact + (jnp.log(nf / max_exact) / math.log(REL_MAX_DIST / max_exact)
                         * (REL_BUCKETS - max_exact)).astype(jnp.int32)
    large = jnp.minimum(large, REL_BUCKETS - 1)
    return jnp.where(n < max_exact, n, large)


def diff_attention(x, w_qkv, w_o, lam_q1, lam_k1, lam_q2, lam_k2, subln_g, rel_bias, lam_init):
    B, S, _ = x.shape
    H, d = DA_HEADS, DA_HEAD_DIM
    q, k, v = jnp.split(x @ w_qkv, 3, axis=-1)
    q = q.reshape(B, S, H, 2, d) * (d ** -0.5)
    k = k.reshape(B, S, H, 2, d)
    v = v.reshape(B, S, H, DA_V_DIM)
    lam = (jnp.exp(jnp.sum(lam_q1 * lam_k1).astype(jnp.float32))
           - jnp.exp(jnp.sum(lam_q2 * lam_k2).astype(jnp.float32)) + lam_init)
    n_blocks = S // Q_BLOCK
    q_blocks = jnp.moveaxis(q.reshape(B, n_blocks, Q_BLOCK, H, 2, d), 1, 0)
    k_pos = jnp.arange(S)

    def attend_block(args):
        q_blk, blk = args
        q_pos = blk * Q_BLOCK + jnp.arange(Q_BLOCK)
        rel = q_pos[:, None] - k_pos[None, :]
        bias = jnp.moveaxis(rel_bias[t5_bucket(rel)], -1, 0).astype(jnp.float32)
        logits = jnp.einsum('bqhmd,bkhmd->bhmqk', q_blk, k).astype(jnp.float32) + bias[None, :, None]
        logits = jnp.where(rel >= 0, logits, -jnp.inf)
        probs = jax.nn.softmax(logits, axis=-1)
        attn = probs[:, :, 0] - lam * probs[:, :, 1]
        return jnp.einsum('bhqk,bkhe->bqhe', attn.astype(v.dtype), v)

    o = lax.map(attend_block, (q_blocks, jnp.arange(n_blocks)))
    o = jnp.moveaxis(o, 0, 1).reshape(B, S, H, DA_V_DIM)
    o = rms_norm(o, subln_g, DA_EPS) * (1.0 - lam_init)
    return o.reshape(B, S, H * DA_V_DIM) @ w_o


def gla(x, w_in, w_a1, w_a2, b_a, norm_g, w_o):
    B, S, _ = x.shape
    H, dk, dv, C = GLA_HEADS, GLA_DK, GLA_DV, GLA_CHUNK
    q, k, v, r = jnp.split(x @ w_in, [H * dk, 2 * H * dk, 2 * H * dk + H * dv], axis=-1)
    log_a = jax.nn.log_sigmoid((x @ w_a1) @ w_a2 + b_a) / GLA_TAU
    nC = S // C

    def to_chunks(t, e):
        return jnp.transpose(t.reshape(B, nC, C, H, e), (0, 3, 1, 2, 4)).astype(jnp.float32)

    q = to_chunks(q, dk) * (dk ** -0.5)
    k = to_chunks(k, dk)
    v = to_chunks(v, dv)
    b = jnp.cumsum(to_chunks(log_a, dk), axis=3)
    b_last = b[:, :, :, -1:]
    q_dec = q * jnp.exp(b)
    att = jnp.einsum('bhnid,bhnjd->bhnij', q_dec, k * jnp.exp(-b))
    att = jnp.where(jnp.tril(jnp.ones((C, C), bool)), att, 0.0)
    o_intra = jnp.einsum('bhnij,bhnje->bhnie', att, v)
    kv_chunk = jnp.einsum('bhnjd,bhnje->bhnde', k * jnp.exp(b_last - b), v)
    decay_chunk = jnp.exp(b_last[:, :, :, 0])

    def step(state, inp):
        kv_c, dec_c = inp
        return state * dec_c[..., None] + kv_c, state

    _, states_prev = lax.scan(step, jnp.zeros((B, H, dk, dv), jnp.float32),
                              (jnp.moveaxis(kv_chunk, 2, 0), jnp.moveaxis(decay_chunk, 2, 0)))
    states_prev = jnp.moveaxis(states_prev, 0, 2)
    o = o_intra + jnp.einsum('bhnid,bhnde->bhnie', q_dec, states_prev)
    o = jnp.transpose(o, (0, 2, 3, 1, 4)).reshape(B, S, H, dv)
    o = rms_norm(o, norm_g, GLA_EPS).astype(x.dtype) * jax.nn.silu(r).reshape(B, S, H, dv)
    return o.reshape(B, S, H * dv) @ w_o


def rwkv7_time_mix(x, mix, w_rkv, w0, w1, w2, a0, a1, a2, g1, g2, k_k, k_a, r_k, ln_g, ln_b, w_o):
    B, S, D = x.shape
    H, N = RW_HEADS, RW_HEAD
    xx = jnp.pad(x, ((0, 0), (1, 0), (0, 0)))[:, :-1] - x
    xm = x[:, :, None, :] + xx[:, :, None, :] * mix
    rkv = jnp.einsum('bsgd,gde->bsge', xm[:, :, :3], w_rkv)
    r, k, v = rkv[:, :, 0], rkv[:, :, 1], rkv[:, :, 2]
    xw, xa, xg = xm[:, :, 3], xm[:, :, 4], xm[:, :, 5]
    w_log = -jax.nn.softplus(-(w0 + jnp.tanh(xw @ w1) @ w2)) - 0.5
    decay = jnp.exp(-jnp.exp(w_log.astype(jnp.float32)))
    a = jax.nn.sigmoid(a0 + (xa @ a1) @ a2)
    g = jax.nn.sigmoid(xg @ g1) @ g2
    kk = (k * k_k).reshape(B, S, H, N).astype(jnp.float32)
    kk = kk * lax.rsqrt(jnp.maximum(jnp.sum(kk * kk, -1, keepdims=True), 1e-24))
    k = k * (1.0 + (a - 1.0) * k_a)

    def heads(t):
        return t.reshape(B, S, H, N).astype(jnp.float32)

    r_h, k_h, v_h, a_h, w_h = heads(r), heads(k), heads(v), heads(a), heads(decay)

    def step(state, inp):
        r_t, w_t, k_t, v_t, kk_t, a_t = inp
        sa = jnp.einsum('bhvk,bhk->bhv', state, -kk_t)
        state = (state * w_t[:, :, None, :] + sa[..., None] * (kk_t * a_t)[:, :, None, :]
                 + v_t[..., None] * k_t[:, :, None, :])
        return state, jnp.einsum('bhvk,bhk->bhv', state, r_t)

    seq_inputs = (jnp.moveaxis(r_h, 1, 0), jnp.moveaxis(w_h, 1, 0), jnp.moveaxis(k_h, 1, 0),
                  jnp.moveaxis(v_h, 1, 0), jnp.moveaxis(kk, 1, 0), jnp.moveaxis(a_h, 1, 0))
    _, y = lax.scan(step, jnp.zeros((B, H, N, N), jnp.float32), seq_inputs)
    y = jnp.moveaxis(y, 0, 1)
    y = layer_norm(y, ln_g.reshape(H, N), ln_b.reshape(H, N), RW_GN_EPS)
    y = y + jnp.sum(r_h * k_h * r_k, -1, keepdims=True) * v_h
    return (y.reshape(B, S, D).astype(x.dtype) * g) @ w_o


def sq_relu_mlp(x, w_up, w_down):
    return jnp.square(jax.nn.relu(x @ w_up)) @ w_down


def setup_inputs(seed: int = 0) -> dict:
    key = jax.random.key(seed)
    ks = iter(jax.random.split(key, 64))
    D = D_MODEL

    def nrm(shape, scale):
        return jax.random.normal(next(ks), shape, jnp.float32) * scale

    x = nrm((BATCH, SEQ, D), 1.0)
    p = nrm((DEPTH, BATCH, SEQ, PLE_DIM), 1.0)
    rel_bias = nrm((REL_BUCKETS, DA_HEADS), 0.5)
    ln1_g = 1.0 + nrm((DEPTH, D), 0.02)
    ln1_b = nrm((DEPTH, D), 0.02)
    ln2_g = 1.0 + nrm((DEPTH, D), 0.02)
    ln2_b = nrm((DEPTH, D), 0.02)
    mlp_up = nrm((DEPTH, D, D_FF), D ** -0.5)
    mlp_down = nrm((DEPTH, D_FF, D), D_FF ** -0.5 * DEEP_BETA)
    ple_proj = nrm((DEPTH, PLE_DIM, D), PLE_DIM ** -0.5)
    ple_gate = nrm((DEPTH, D, D), D ** -0.5)
    da_w_qkv = jnp.concatenate([nrm((N_A, D, 2 * D), D ** -0.5),
                                nrm((N_A, D, DA_HEADS * DA_V_DIM), D ** -0.5 * DEEP_BETA)], axis=-1)
    da_w_o = nrm((N_A, DA_HEADS * DA_V_DIM, D), D ** -0.5 * DEEP_BETA)
    da_lam_q1 = nrm((N_A, DA_HEAD_DIM), 0.1)
    da_lam_k1 = nrm((N_A, DA_HEAD_DIM), 0.1)
    da_lam_q2 = nrm((N_A, DA_HEAD_DIM), 0.1)
    da_lam_k2 = nrm((N_A, DA_HEAD_DIM), 0.1)
    da_subln_g = 1.0 + nrm((N_A, DA_V_DIM), 0.02)
    gla_w_in = jnp.concatenate([nrm((N_B, D, 2 * GLA_HEADS * GLA_DK), D ** -0.5),
                                nrm((N_B, D, GLA_HEADS * GLA_DV), D ** -0.5 * DEEP_BETA),
                                nrm((N_B, D, GLA_HEADS * GLA_DV), D ** -0.5)], axis=-1)
    gla_w_a1 = nrm((N_B, D, GLA_GATE_RANK), D ** -0.5)
    gla_w_a2 = nrm((N_B, GLA_GATE_RANK, GLA_HEADS * GLA_DK), GLA_GATE_RANK ** -0.5)
    gla_b_a = nrm((N_B, GLA_HEADS * GLA_DK), 0.1)
    gla_norm_g = 1.0 + nrm((N_B, GLA_DV), 0.02)
    gla_w_o = nrm((N_B, GLA_HEADS * GLA_DV, D), D ** -0.5 * DEEP_BETA)
    rw_mix = jax.random.uniform(next(ks), (N_C, 6, D), jnp.float32)
    rw_w_rkv = nrm((N_C, 3, D, D), D ** -0.5) * jnp.array([1.0, 1.0, DEEP_BETA], jnp.float32)[None, :, None, None]
    rw_w0 = -1.0 + nrm((N_C, D), 0.5)
    rw_w1 = nrm((N_C, D, RW_DECAY_LORA), D ** -0.5)
    rw_w2 = nrm((N_C, RW_DECAY_LORA, D), 0.1)
    rw_a0 = nrm((N_C, D), 0.1)
    rw_a1 = nrm((N_C, D, RW_A_LORA), D ** -0.5)
    rw_a2 = nrm((N_C, RW_A_LORA, D), 0.1)
    rw_g1 = nrm((N_C, D, RW_GATE_LORA), D ** -0.5)
    rw_g2 = nrm((N_C, RW_GATE_LORA, D), RW_GATE_LORA ** -0.5)
    rw_k_k = 0.85 + nrm((N_C, D), 0.02)
    rw_k_a = 1.0 + nrm((N_C, D), 0.02)
    rw_r_k = nrm((N_C, RW_HEADS, RW_HEAD), 0.1)
    rw_ln_g = 1.0 + nrm((N_C, D), 0.02)
    rw_ln_b = nrm((N_C, D), 0.02)
    rw_w_o = nrm((N_C, D, D), D ** -0.5 * DEEP_BETA)
    return {'x': x, 'p': p, 'rel_bias': rel_bias,
            'ln1_g': ln1_g, 'ln1_b': ln1_b, 'ln2_g': ln2_g, 'ln2_b': ln2_b,
            'mlp_up': mlp_up, 'mlp_down': mlp_down, 'ple_proj': ple_proj, 'ple_gate': ple_gate,
            'da_w_qkv': da_w_qkv, 'da_w_o': da_w_o, 'da_lam_q1': da_lam_q1, 'da_lam_k1': da_lam_k1,
            'da_lam_q2': da_lam_q2, 'da_lam_k2': da_lam_k2, 'da_subln_g': da_subln_g,
            'gla_w_in': gla_w_in, 'gla_w_a1': gla_w_a1, 'gla_w_a2': gla_w_a2, 'gla_b_a': gla_b_a,
            'gla_norm_g': gla_norm_g, 'gla_w_o': gla_w_o,
            'rw_mix': rw_mix, 'rw_w_rkv': rw_w_rkv, 'rw_w0': rw_w0, 'rw_w1': rw_w1, 'rw_w2': rw_w2,
            'rw_a0': rw_a0, 'rw_a1': rw_a1, 'rw_a2': rw_a2, 'rw_g1': rw_g1, 'rw_g2': rw_g2,
            'rw_k_k': rw_k_k, 'rw_k_a': rw_k_a, 'rw_r_k': rw_r_k, 'rw_ln_g': rw_ln_g, 'rw_ln_b': rw_ln_b,
            'rw_w_o': rw_w_o}


def reference(x, p, rel_bias, ln1_g, ln1_b, ln2_g, ln2_b, mlp_up, mlp_down, ple_proj, ple_gate,
              da_w_qkv, da_w_o, da_lam_q1, da_lam_k1, da_lam_q2, da_lam_k2, da_subln_g,
              gla_w_in, gla_w_a1, gla_w_a2, gla_b_a, gla_norm_g, gla_w_o,
              rw_mix, rw_w_rkv, rw_w0, rw_w1, rw_w2, rw_a0, rw_a1, rw_a2, rw_g1, rw_g2,
              rw_k_k, rw_k_a, rw_r_k, rw_ln_g, rw_ln_b, rw_w_o):
    h = x
    for i in range(DEPTH):
        kind, j = i % N_MIXERS, i // N_MIXERS
        if kind == 0:
            lam_init = 0.8 - 0.6 * math.exp(-0.3 * i)
            mixed = diff_attention(h, da_w_qkv[j], da_w_o[j], da_lam_q1[j], da_lam_k1[j],
                                   da_lam_q2[j], da_lam_k2[j], da_subln_g[j], rel_bias, lam_init)
        elif kind == 1:
            mixed = gla(h, gla_w_in[j], gla_w_a1[j], gla_w_a2[j], gla_b_a[j], gla_norm_g[j], gla_w_o[j])
        else:
            mixed = rwkv7_time_mix(h, rw_mix[j], rw_w_rkv[j], rw_w0[j], rw_w1[j], rw_w2[j],
                                   rw_a0[j], rw_a1[j], rw_a2[j], rw_g1[j], rw_g2[j],
                                   rw_k_k[j], rw_k_a[j], rw_r_k[j], rw_ln_g[j], rw_ln_b[j], rw_w_o[j])
        h = layer_norm(DEEP_ALPHA * h + mixed, ln1_g[i], ln1_b[i], LN_EPS)
        h = layer_norm(DEEP_ALPHA * h + sq_relu_mlp(h, mlp_up[i], mlp_down[i]), ln2_g[i], ln2_b[i], LN_EPS)
        h = h + jax.nn.sigmoid(h @ ple_gate[i]) * (p[i] @ ple_proj[i])
    return h
```

```python
import functools
import math

import jax
import jax.numpy as jnp
import numpy as np
from jax import lax
from jax.experimental import pallas as pl
from jax.experimental.pallas import tpu as pltpu

D_MODEL = 1024
DEPTH = 4
N_MIXERS = 3
PLE_DIM = 256
D_FF = 4 * D_MODEL
DEEP_ALPHA = (2.0 * DEPTH) ** 0.25
LN_EPS = 1e-5

REL_BUCKETS = 32
REL_MAX_DIST = 128

DA_HEADS = D_MODEL // 128
DA_HEAD_DIM = 64
DA_V_DIM = 2 * DA_HEAD_DIM
DA_EPS = 1e-5

GLA_HEADS = 4
GLA_DK = D_MODEL // 2 // GLA_HEADS
GLA_DV = D_MODEL // GLA_HEADS
GLA_GATE_RANK = 16
GLA_TAU = 16.0
GLA_CHUNK = 64
GLA_EPS = 1e-5

RW_HEAD = 64
RW_HEADS = D_MODEL // RW_HEAD
RW_GN_EPS = 64e-5
RW_CHUNK = 64

LANES = 128
MASK_VALUE = -1e30
VMEM_LIMIT = 48 * 1024 * 1024

F32 = jnp.float32
BF16 = jnp.bfloat16
HI = lax.Precision.HIGHEST


def _bf(x):
    return x.astype(BF16)


def _dot(a, b, precision=None):
    return jnp.dot(a, b, preferred_element_type=F32, precision=precision)


def _dot_nt(a, b, precision=None):
    return lax.dot_general(a, b, (((1,), (1,)), ((), ())),
                           preferred_element_type=F32, precision=precision)


def _dot_tn(a, b, precision=None):
    return lax.dot_general(a, b, (((0,), (0,)), ((), ())),
                           preferred_element_type=F32, precision=precision)


def _layer_norm(z, g, b, eps):
    mu = jnp.mean(z, -1, keepdims=True)
    zc = z - mu
    var = jnp.mean(zc * zc, -1, keepdims=True)
    return zc * lax.rsqrt(var + eps) * g + b


def _sigmoid(x):
    return 1.0 / (1.0 + jnp.exp(-x))


def _softplus(x):
    return jnp.maximum(x, 0.0) + jnp.log(1.0 + jnp.exp(-jnp.abs(x)))


def _params(*sem):
    return pltpu.CompilerParams(dimension_semantics=sem, vmem_limit_bytes=VMEM_LIMIT)


def _proj_kernel(x_ref, w_ref, o_ref):
    o_ref[...] = _dot(_bf(x_ref[...]), w_ref[...]).astype(o_ref.dtype)


def _proj(x, w, out_dtype, tm=1024, tn=1024):
    T, K = x.shape
    N = w.shape[1]
    tn = min(tn, N)
    return pl.pallas_call(
        _proj_kernel,
        grid=(T // tm, N // tn),
        in_specs=[pl.BlockSpec((tm, K), lambda i, j: (i, 0)),
                  pl.BlockSpec((K, tn), lambda i, j: (0, j))],
        out_specs=pl.BlockSpec((tm, tn), lambda i, j: (i, j)),
        out_shape=jax.ShapeDtypeStruct((T, N), out_dtype),
        compiler_params=_params("parallel", "parallel"),
        name="proj",
    )(x, w)


def _proj_ln_kernel(x_ref, w_ref, h_ref, g_ref, b_ref, o_ref):
    y = _dot(_bf(x_ref[...]), w_ref[...])
    o_ref[...] = _layer_norm(DEEP_ALPHA * h_ref[...] + y, g_ref[...], b_ref[...], LN_EPS)


def _proj_ln(x, w, h, g, b, tm=512):
    T, K = x.shape
    D = w.shape[1]
    return pl.pallas_call(
        _proj_ln_kernel,
        grid=(T // tm,),
        in_specs=[pl.BlockSpec((tm, K), lambda i: (i, 0)),
                  pl.BlockSpec((K, D), lambda i: (0, 0)),
                  pl.BlockSpec((tm, D), lambda i: (i, 0)),
                  pl.BlockSpec((1, D), lambda i: (0, 0)),
                  pl.BlockSpec((1, D), lambda i: (0, 0))],
        out_specs=pl.BlockSpec((tm, D), lambda i: (i, 0)),
        out_shape=jax.ShapeDtypeStruct((T, D), F32),
        compiler_params=_params("parallel"),
        name="proj_ln",
    )(x, w, h, g.reshape(1, D), b.reshape(1, D))


def _mlp_kernel(h_ref, wu_ref, wd_ref, g_ref, b_ref, p_ref, wg_ref, wp_ref, o_ref,
                xb_ref, acc_ref):
    k = pl.program_id(1)

    @pl.when(k == 0)
    def _():
        xb_ref[...] = _bf(h_ref[...])

    a = _dot(xb_ref[...], wu_ref[...])
    a = jnp.square(jnp.maximum(a, 0.0))
    part = _dot(_bf(a), wd_ref[...])

    @pl.when(k == 0)
    def _():
        acc_ref[...] = part

    @pl.when(k > 0)
    def _():
        acc_ref[...] += part

    @pl.when(k == pl.num_programs(1) - 1)
    def _():
        h2 = _layer_norm(DEEP_ALPHA * h_ref[...] + acc_ref[...], g_ref[...], b_ref[...], LN_EPS)
        gate = _sigmoid(_dot(_bf(h2), wg_ref[...]))
        pp = _dot(_bf(p_ref[...]), wp_ref[...])
        o_ref[...] = h2 + gate * pp


def _mlp(h, w_up, w_down, g, b, p, w_gate, w_proj, tm=512, tf=1024):
    T, D = h.shape
    FF = w_up.shape[1]
    P = p.shape[1]
    return pl.pallas_call(
        _mlp_kernel,
        grid=(T // tm, FF // tf),
        in_specs=[pl.BlockSpec((tm, D), lambda i, k: (i, 0)),
                  pl.BlockSpec((D, tf), lambda i, k: (0, k)),
                  pl.BlockSpec((tf, D), lambda i, k: (k, 0)),
                  pl.BlockSpec((1, D), lambda i, k: (0, 0)),
                  pl.BlockSpec((1, D), lambda i, k: (0, 0)),
                  pl.BlockSpec((tm, P), lambda i, k: (i, 0)),
                  pl.BlockSpec((D, D), lambda i, k: (0, 0)),
                  pl.BlockSpec((P, D), lambda i, k: (0, 0))],
        out_specs=pl.BlockSpec((tm, D), lambda i, k: (i, 0)),
        out_shape=jax.ShapeDtypeStruct((T, D), F32),
        scratch_shapes=[pltpu.VMEM((tm, D), BF16), pltpu.VMEM((tm, D), F32)],
        compiler_params=_params("parallel", "arbitrary"),
        name="mlp",
    )(h, w_up, w_down, g.reshape(1, D), b.reshape(1, D), p, w_gate, w_proj)


def _bucket_tiles(t):
    i = np.arange(t)[:, None]
    j = np.arange(t)[None, :]
    out = []
    for d in range(2):
        rel = i - j + d * t
        n = np.maximum(rel, 0)
        max_exact = REL_BUCKETS // 2
        nf = np.maximum(n, 1).astype(np.float32)
        large = max_exact + (np.log(nf / np.float32(max_exact))
                             / np.float32(math.log(REL_MAX_DIST / max_exact))
                             * np.float32(REL_BUCKETS - max_exact)).astype(np.int32)
        large = np.minimum(large, REL_BUCKETS - 1)
        bucket = np.where(n < max_exact, n, large)
        out.append(np.where(rel >= 0, bucket, -1))
    return np.stack(out).astype(np.int32)


def _bias_tiles_kernel(rb_ref, bucket_ref, o_ref):
    h = pl.program_id(0)
    bucket = bucket_ref[...]
    far = rb_ref[REL_BUCKETS - 1, h]
    acc = jnp.zeros(bucket.shape, F32)
    for b in range(REL_BUCKETS):
        acc = jnp.where(bucket == b, rb_ref[b, h] - far, acc)
    o_ref[0] = jnp.where(bucket < 0, MASK_VALUE, acc)


def _bias_tiles(rel_bias, t):
    H = rel_bias.shape[1]
    buckets = jnp.asarray(_bucket_tiles(t))
    return pl.pallas_call(
        _bias_tiles_kernel,
        grid=(H,),
        in_specs=[pl.BlockSpec(memory_space=pltpu.SMEM),
                  pl.BlockSpec((2, t, t), lambda h: (0, 0, 0))],
        out_specs=pl.BlockSpec((1, 2, t, t), lambda h: (h, 0, 0, 0)),
        out_shape=jax.ShapeDtypeStruct((H, 2, t, t), F32),
        compiler_params=_params("parallel"),
        name="bias_tiles",
    )(rel_bias, buckets)


def _attn_kernel(q_ref, k_ref, v_ref, bias_ref, lam_ref, sg_ref, o_ref, *, t, lam_init):
    qi = pl.program_id(2)
    d = DA_HEAD_DIM
    q = q_ref[0] * jnp.asarray(d ** -0.5, BF16)
    lane = lax.broadcasted_iota(jnp.int32, q.shape, 1)
    zero = jnp.zeros_like(q)
    qs = (jnp.where(lane < d, q, zero), jnp.where(lane >= d, q, zero))

    def step(j, carry, tile):
        rows = pl.ds(pl.multiple_of(j * t, t), t)
        k = k_ref[0, rows, :]
        v = v_ref[0, rows, :]
        new = []
        for mi in range(2):
            m_old, l_old, acc_old = carry[mi]
            s = _dot_nt(qs[mi], k)
            if tile is not None:
                s = s + bias_ref[0, tile]
            m_new = jnp.maximum(m_old, jnp.max(s, -1, keepdims=True))
            p = jnp.exp(s - m_new)
            alpha = jnp.exp(m_old - m_new)
            l_new = alpha * l_old + jnp.sum(p, -1, keepdims=True)
            acc_new = alpha * acc_old + _dot(_bf(p), v)
            new.append((m_new, l_new, acc_new))
        return tuple(new)

    init_map = (jnp.full((t, 1), MASK_VALUE, F32), jnp.zeros((t, 1), F32),
                jnp.zeros((t, DA_V_DIM), F32))
    carry = (init_map, init_map)
    carry = lax.fori_loop(0, jnp.maximum(qi - 1, 0), lambda j, c: step(j, c, None), carry)
    carry = lax.cond(qi > 0, lambda c: step(qi - 1, c, 1), lambda c: c, carry)
    carry = step(qi, carry, 0)

    lv = lam_ref[...]
    lam = (jnp.exp(jnp.sum(lv[0:1] * lv[1:2], -1, keepdims=True))
           - jnp.exp(jnp.sum(lv[2:3] * lv[3:4], -1, keepdims=True)) + lam_init)
    (_, l0, acc0), (_, l1, acc1) = carry
    o = acc0 / l0 - lam * (acc1 / l1)
    o = o * lax.rsqrt(jnp.mean(o * o, -1, keepdims=True) + DA_EPS) * sg_ref[...]
    o_ref[0] = (o * (1.0 - lam_init)).astype(o_ref.dtype)


def _diff_attention_core(qkv, bias_tiles, lam_vecs, subln_g, lam_init, t):
    B, S, _ = qkv.shape
    H = DA_HEADS
    W = 2 * DA_HEAD_DIM
    return pl.pallas_call(
        functools.partial(_attn_kernel, t=t, lam_init=lam_init),
        grid=(B, H, S // t),
        in_specs=[pl.BlockSpec((1, t, W), lambda b, h, i: (b, i, h)),
                  pl.BlockSpec((1, S, W), lambda b, h, i: (b, 0, H + h)),
                  pl.BlockSpec((1, S, W), lambda b, h, i: (b, 0, 2 * H + h)),
                  pl.BlockSpec((1, 2, t, t), lambda b, h, i: (h, 0, 0, 0)),
                  pl.BlockSpec((4, DA_HEAD_DIM), lambda b, h, i: (0, 0)),
                  pl.BlockSpec((1, DA_V_DIM), lambda b, h, i: (0, 0))],
        out_specs=pl.BlockSpec((1, t, W), lambda b, h, i: (b, i, h)),
        out_shape=jax.ShapeDtypeStruct((B, S, D_MODEL), BF16),
        compiler_params=_params("parallel", "parallel", "arbitrary"),
        name="diff_attn",
    )(qkv, qkv, qkv, bias_tiles, lam_vecs, subln_g.reshape(1, DA_V_DIM))


def _gla_kernel(q_ref, k_ref, v_ref, r_ref, za_ref, wa2_ref, ba_ref, ng_ref, o_ref, s_ref, *, L):
    C = GLA_CHUNK

    @pl.when(pl.program_id(2) == 0)
    def _():
        s_ref[...] = jnp.zeros_like(s_ref)

    row = lax.broadcasted_iota(jnp.int32, (C, C), 0)
    col = lax.broadcasted_iota(jnp.int32, (C, C), 1)
    tril = row >= col
    tril_f = tril.astype(F32)
    ones = jnp.ones((C, GLA_DV), F32)

    def chunk(c, _):
        rows = pl.ds(pl.multiple_of(c * C, C), C)
        z = _dot(za_ref[0, rows, :], wa2_ref[...], HI) + ba_ref[...]
        log_a = -_softplus(-z) * (1.0 / GLA_TAU)
        b = _dot(tril_f, log_a, HI)
        q = q_ref[0, rows, :].astype(F32) * (GLA_DK ** -0.5)
        k = k_ref[0, rows, :].astype(F32)
        v = v_ref[0, rows, :]
        q_dec = _bf(q * jnp.exp(b))
        att = _dot_nt(q_dec, _bf(k * jnp.exp(-b)))
        att = jnp.where(tril, att, 0.0)
        state = s_ref[...]
        o = _dot(_bf(att), v) + _dot(q_dec, _bf(state))
        b_last = b[C - 1:C, :]
        k_to_end = _bf(k * jnp.exp(b_last - b))
        decay = jnp.exp(_dot_tn(log_a, ones, HI))
        s_ref[...] = state * decay + _dot_tn(k_to_end, v)
        o = o * lax.rsqrt(jnp.mean(o * o, -1, keepdims=True) + GLA_EPS) * ng_ref[...]
        r = r_ref[0, rows, :].astype(F32)
        o_ref[0, rows, :] = (o * (r * _sigmoid(r))).astype(o_ref.dtype)
        return 0

    lax.fori_loop(0, L // C, chunk, 0)


def _gla_core(qkvr, za, w_a2p, b_a, norm_g, L=512):
    B, S, _ = qkvr.shape
    H, dk, dv = GLA_HEADS, GLA_DK, GLA_DV
    return pl.pallas_call(
        functools.partial(_gla_kernel, L=L),
        grid=(B, H, S // L),
        in_specs=[pl.BlockSpec((1, L, dk), lambda b, h, l: (b, l, h)),
                  pl.BlockSpec((1, L, dk), lambda b, h, l: (b, l, H + h)),
                  pl.BlockSpec((1, L, dv), lambda b, h, l: (b, l, H + h)),
                  pl.BlockSpec((1, L, dv), lambda b, h, l: (b, l, 2 * H + h)),
                  pl.BlockSpec((1, L, LANES), lambda b, h, l: (b, l, 0)),
                  pl.BlockSpec((LANES, dk), lambda b, h, l: (0, h)),
                  pl.BlockSpec((1, dk), lambda b, h, l: (0, h)),
                  pl.BlockSpec((1, dv), lambda b, h, l: (0, 0))],
        out_specs=pl.BlockSpec((1, L, dv), lambda b, h, l: (b, l, h)),
        out_shape=jax.ShapeDtypeStruct((B, S, D_MODEL), BF16),
        scratch_shapes=[pltpu.VMEM((dk, dv), F32)],
        compiler_params=_params("parallel", "parallel", "arbitrary"),
        name="gla",
    )(qkvr, qkvr, qkvr, qkvr, za, w_a2p, b_a.reshape(1, H * dk), norm_g.reshape(1, dv))


def _rw_proj_kernel(x_ref, xp_ref, mix_ref, wrkv_ref, w0_ref, w1_ref, w2_ref, a0_ref, a1_ref,
                    a2_ref, g1_ref, g2_ref, r_o, k_o, v_o, lw_o, a_o, g_o, *, tm, S):
    i = pl.program_id(0)
    x = x_ref[...]
    at_start = (i * tm) % S == 0
    prev = jnp.where(at_start, 0.0, xp_ref[7:8, :])
    row = lax.broadcasted_iota(jnp.int32, x.shape, 0)
    x_prev = jnp.where(row == 0, prev, pltpu.roll(x, 1, 0))
    xx = x_prev - x

    def mixed(gi):
        return _bf(x + xx * mix_ref[gi:gi + 1, :])

    r_o[...] = _dot(mixed(0), wrkv_ref[0])
    k_o[...] = _dot(mixed(1), wrkv_ref[1])
    v_o[...] = _dot(mixed(2), wrkv_ref[2])
    u = w0_ref[...] + _dot(_bf(jnp.tanh(_dot(mixed(3), w1_ref[...]))), w2_ref[...])
    lw_o[...] = -jnp.exp(-_softplus(-u) - 0.5)
    a_o[...] = _sigmoid(a0_ref[...] + _dot(_bf(_dot(mixed(4), a1_ref[...])), a2_ref[...]))
    g_o[...] = _dot(_bf(_sigmoid(_dot(mixed(5), g1_ref[...]))), g2_ref[...])


def _rw_proj(h, S, mix, w_rkv, w0, w1, w2, a0, a1, a2, g1, g2, tm=256):
    T, D = h.shape
    full = lambda a: pl.BlockSpec(a.shape, lambda i: (0,) * a.ndim)
    w0, a0 = w0.reshape(1, D), a0.reshape(1, D)
    out = jax.ShapeDtypeStruct((T, D), F32)
    tile = pl.BlockSpec((tm, D), lambda i: (i, 0))
    return pl.pallas_call(
        functools.partial(_rw_proj_kernel, tm=tm, S=S),
        grid=(T // tm,),
        in_specs=[tile,
                  pl.BlockSpec((8, D), lambda i: (jnp.maximum(i * (tm // 8) - 1, 0), 0)),
                  full(mix), full(w_rkv), full(w0), full(w1), full(w2), full(a0), full(a1),
                  full(a2), full(g1), full(g2)],
        out_specs=[tile] * 6,
        out_shape=[out] * 6,
        compiler_params=_params("parallel"),
        name="rw_proj",
    )(h, h, mix, w_rkv, w0, w1, w2, a0, a1, a2, g1, g2)


def _rw_scan_kernel(r_ref, k_ref, v_ref, a_ref, lw_ref, g_ref, kk_ref, ka_ref, rk_ref, lng_ref,
                    lnb_ref, o_ref, h_ref, *, L):
    C = RW_CHUNK
    N = RW_HEAD

    @pl.when(pl.program_id(2) == 0)
    def _():
        h_ref[...] = jnp.zeros_like(h_ref)

    row = lax.broadcasted_iota(jnp.int32, (C, C), 0)
    col = lax.broadcasted_iota(jnp.int32, (C, C), 1)
    incl = row >= col
    strict = row > col
    incl_f = incl.astype(F32)
    eye = row == col
    eye_f = eye.astype(F32)

    def chunk(c, _):
        rows = pl.ds(pl.multiple_of(c * C, C), C)
        r2, k2, v2, a2, lw2, g2 = (ref[0, rows, :] for ref in (r_ref, k_ref, v_ref, a_ref, lw_ref, g_ref))
        outs = []
        for hd in range(2):
            ls = slice(hd * N, (hd + 1) * N)
            r, k, v, a, lw = r2[:, ls], k2[:, ls], v2[:, ls], a2[:, ls], lw2[:, ls]
            kk = k * kk_ref[:, ls]
            kk = kk * lax.rsqrt(jnp.maximum(jnp.sum(kk * kk, -1, keepdims=True), 1e-24))
            bb = kk * a
            k = k * (1.0 + (a - 1.0) * ka_ref[:, ls])
            gc = _dot(incl_f, lw, HI)
            ge = gc - lw
            g_last = gc[C - 1:C, :]
            a_g = -kk * jnp.exp(ge)
            r_g = r * jnp.exp(gc)
            inv = jnp.exp(-gc)
            b_t = bb * inv
            k_t = k * inv
            to_end = jnp.exp(g_last - gc)
            b_d = bb * to_end
            k_d = k * to_end
            a_ab = jnp.where(strict, _dot_nt(a_g, b_t, HI), 0.0)
            a_ak = jnp.where(strict, _dot_nt(a_g, k_t, HI), 0.0)
            a_rb = jnp.where(incl, _dot_nt(r_g, b_t, HI), 0.0)
            a_rk = jnp.where(incl, _dot_nt(r_g, k_t, HI), 0.0)
            t_inv = eye_f + a_ab
            pw = a_ab
            for _ in range(int(math.log2(C)) - 1):
                pw = _dot(pw, pw, HI)
                t_inv = t_inv + _dot(t_inv, pw, HI)
            w1 = _dot(t_inv, a_g, HI)
            w2 = _dot(t_inv, _dot(a_ak, v, HI), HI)
            r_q = r_g + _dot(a_rb, w1, HI)
            y0 = _dot(a_rb, w2, HI) + _dot(a_rk, v, HI)
            m = jnp.where(eye, jnp.exp(g_last), 0.0) + _dot_tn(b_d, w1, HI)
            n = _dot_tn(b_d, w2, HI) + _dot_tn(k_d, v, HI)
            h0 = h_ref[hd]
            y = _dot(r_q, h0, HI) + y0
            h_ref[hd] = _dot(m, h0, HI) + n
            y = _layer_norm(y, lng_ref[:, ls], lnb_ref[:, ls], RW_GN_EPS)
            y = y + jnp.sum(r * k * rk_ref[:, ls], -1, keepdims=True) * v
            outs.append(y)
        o_ref[0, rows, :] = (jnp.concatenate(outs, -1) * g2).astype(o_ref.dtype)
        return 0

    lax.fori_loop(0, L // C, chunk, 0)


def _rw_scan(r, k, v, a, lw, g, k_k, k_a, r_k, ln_g, ln_b, L=512):
    B, S, D = r.shape
    W = 2 * RW_HEAD
    seq = pl.BlockSpec((1, L, W), lambda b, h, l: (b, l, h))
    par = pl.BlockSpec((1, W), lambda b, h, l: (0, h))
    vec = lambda p: p.reshape(1, D)
    return pl.pallas_call(
        functools.partial(_rw_scan_kernel, L=L),
        grid=(B, D // W, S // L),
        in_specs=[seq] * 6 + [par] * 5,
        out_specs=seq,
        out_shape=jax.ShapeDtypeStruct((B, S, D), BF16),
        scratch_shapes=[pltpu.VMEM((2, RW_HEAD, RW_HEAD), F32)],
        compiler_params=_params("parallel", "parallel", "arbitrary"),
        name="rw_scan",
    )(r, k, v, a, lw, g, vec(k_k), vec(k_a), vec(r_k), vec(ln_g), vec(ln_b))


def kernel(x, p, rel_bias, ln1_g, ln1_b, ln2_g, ln2_b, mlp_up, mlp_down, ple_proj, ple_gate, da_w_qkv, da_w_o, da_lam_q1, da_lam_k1, da_lam_q2, da_lam_k2, da_subln_g, gla_w_in, gla_w_a1, gla_w_a2, gla_b_a, gla_norm_g, gla_w_o, rw_mix, rw_w_rkv, rw_w0, rw_w1, rw_w2, rw_a0, rw_a1, rw_a2, rw_g1, rw_g2, rw_k_k, rw_k_a, rw_r_k, rw_ln_g, rw_ln_b, rw_w_o):
    B, S, D = x.shape
    T = B * S
    attn_tile = 256
    h = x.reshape(T, D)
    bias_tiles = _bias_tiles(rel_bias, attn_tile)
    for i in range(DEPTH):
        kind, j = i % N_MIXERS, i // N_MIXERS
        if kind == 0:
            lam_init = 0.8 - 0.6 * math.exp(-0.3 * i)
            qkv = _proj(h, _bf(da_w_qkv[j]), BF16).reshape(B, S, 3 * D)
            lam_vecs = jnp.stack([da_lam_q1[j], da_lam_k1[j], da_lam_q2[j], da_lam_k2[j]])
            mixed = _diff_attention_core(qkv, bias_tiles, lam_vecs, da_subln_g[j], lam_init, attn_tile)
            w_o = da_w_o[j]
        elif kind == 1:
            qkvr = _proj(h, _bf(gla_w_in[j]), BF16).reshape(B, S, 3 * D)
            w_a1p = jnp.pad(gla_w_a1[j], ((0, 0), (0, LANES - GLA_GATE_RANK)))
            w_a2p = jnp.pad(gla_w_a2[j], ((0, LANES - GLA_GATE_RANK), (0, 0)))
            za = _proj(h, _bf(w_a1p), F32).reshape(B, S, LANES)
            mixed = _gla_core(qkvr, za, w_a2p, gla_b_a[j], gla_norm_g[j])
            w_o = gla_w_o[j]
        else:
            r, k, v, lw, a, g = _rw_proj(
                h, S, rw_mix[j], _bf(rw_w_rkv[j]), rw_w0[j], _bf(rw_w1[j]), _bf(rw_w2[j]), rw_a0[j],
                _bf(rw_a1[j]), _bf(rw_a2[j]), _bf(rw_g1[j]), _bf(rw_g2[j]))
            sh = lambda t: t.reshape(B, S, D)
            mixed = _rw_scan(sh(r), sh(k), sh(v), sh(a), sh(lw), sh(g), rw_k_k[j], rw_k_a[j],
                             rw_r_k[j].reshape(D), rw_ln_g[j], rw_ln_b[j])
            w_o = rw_w_o[j]
        h = _proj_ln(mixed.reshape(T, D), _bf(w_o), h, ln1_g[i], ln1_b[i])
        h = _mlp(h, _bf(mlp_up[i]), _bf(mlp_down[i]), ln2_g[i], ln2_b[i], p[i].reshape(T, PLE_DIM),
                 _bf(ple_gate[i]), _bf(ple_proj[i]))
    return h.reshape(B, S, D)
```

```python
import functools
import math

import jax
import jax.numpy as jnp
import numpy as np
from jax import lax
from jax.experimental import pallas as pl
from jax.experimental.pallas import tpu as pltpu

D_MODEL = 1024
DEPTH = 4
N_MIXERS = 3
PLE_DIM = 256
D_FF = 4 * D_MODEL
DEEP_ALPHA = (2.0 * DEPTH) ** 0.25
LN_EPS = 1e-5

REL_BUCKETS = 32
REL_MAX_DIST = 128

DA_HEADS = D_MODEL // 128
DA_HEAD_DIM = 64
DA_V_DIM = 2 * DA_HEAD_DIM
DA_EPS = 1e-5

GLA_HEADS = 4
GLA_DK = D_MODEL // 2 // GLA_HEADS
GLA_DV = D_MODEL // GLA_HEADS
GLA_GATE_RANK = 16
GLA_TAU = 16.0
GLA_CHUNK = 64
GLA_EPS = 1e-5

RW_HEAD = 64
RW_HEADS = D_MODEL // RW_HEAD
RW_GN_EPS = 64e-5
RW_CHUNK = 64

LANES = 128
MASK_VALUE = -1e30
VMEM_LIMIT = 48 * 1024 * 1024

F32 = jnp.float32
BF16 = jnp.bfloat16
HI = lax.Precision.HIGHEST


def _bf(x):
    return x.astype(BF16)


def _dot(a, b, precision=None):
    return jnp.dot(a, b, preferred_element_type=F32, precision=precision)


def _dot_nt(a, b, precision=None):
    return lax.dot_general(a, b, (((1,), (1,)), ((), ())),
                           preferred_element_type=F32, precision=precision)


def _dot_tn(a, b, precision=None):
    return lax.dot_general(a, b, (((0,), (0,)), ((), ())),
                           preferred_element_type=F32, precision=precision)


def _layer_norm(z, g, b, eps):
    mu = jnp.mean(z, -1, keepdims=True)
    zc = z - mu
    var = jnp.mean(zc * zc, -1, keepdims=True)
    return zc * lax.rsqrt(var + eps) * g + b


def _sigmoid(x):
    return 1.0 / (1.0 + jnp.exp(-x))


def _softplus(x):
    return jnp.maximum(x, 0.0) + jnp.log(1.0 + jnp.exp(-jnp.abs(x)))


def _params(*sem):
    return pltpu.CompilerParams(dimension_semantics=sem, vmem_limit_bytes=VMEM_LIMIT)


def _proj_kernel(x_ref, w_ref, o_ref):
    o_ref[...] = _dot(_bf(x_ref[...]), w_ref[...]).astype(o_ref.dtype)


def _proj(x, w, out_dtype, tm=1024, tn=1024):
    T, K = x.shape
    N = w.shape[1]
    tn = min(tn, N)
    return pl.pallas_call(
        _proj_kernel,
        grid=(T // tm, N // tn),
        in_specs=[pl.BlockSpec((tm, K), lambda i, j: (i, 0)),
                  pl.BlockSpec((K, tn), lambda i, j: (0, j))],
        out_specs=pl.BlockSpec((tm, tn), lambda i, j: (i, j)),
        out_shape=jax.ShapeDtypeStruct((T, N), out_dtype),
        compiler_params=_params("parallel", "parallel"),
        name="proj",
    )(x, w)


def _proj_ln_kernel(x_ref, w_ref, h_ref, g_ref, b_ref, o_ref):
    y = _dot(_bf(x_ref[...]), w_ref[...])
    o_ref[...] = _layer_norm(DEEP_ALPHA * h_ref[...] + y, g_ref[...], b_ref[...], LN_EPS)


def _proj_ln(x, w, h, g, b, tm=512):
    T, K = x.shape
    D = w.shape[1]
    return pl.pallas_call(
        _proj_ln_kernel,
        grid=(T // tm,),
        in_specs=[pl.BlockSpec((tm, K), lambda i: (i, 0)),
                  pl.BlockSpec((K, D), lambda i: (0, 0)),
                  pl.BlockSpec((tm, D), lambda i: (i, 0)),
                  pl.BlockSpec((1, D), lambda i: (0, 0)),
                  pl.BlockSpec((1, D), lambda i: (0, 0))],
        out_specs=pl.BlockSpec((tm, D), lambda i: (i, 0)),
        out_shape=jax.ShapeDtypeStruct((T, D), F32),
        compiler_params=_params("parallel"),
        name="proj_ln",
    )(x, w, h, g.reshape(1, D), b.reshape(1, D))


def _mlp_kernel(h_ref, wu_ref, wd_ref, g_ref, b_ref, p_ref, wg_ref, wp_ref, o_ref,
                xb_ref, acc_ref):
    k = pl.program_id(1)

    @pl.when(k == 0)
    def _():
        xb_ref[...] = _bf(h_ref[...])

    a = _dot(xb_ref[...], wu_ref[...])
    a = jnp.square(jnp.maximum(a, 0.0))
    part = _dot(_bf(a), wd_ref[...])

    @pl.when(k == 0)
    def _():
        acc_ref[...] = part

    @pl.when(k > 0)
    def _():
        acc_ref[...] += part

    @pl.when(k == pl.num_programs(1) - 1)
    def _():
        h2 = _layer_norm(DEEP_ALPHA * h_ref[...] + acc_ref[...], g_ref[...], b_ref[...], LN_EPS)
        gate = _sigmoid(_dot(_bf(h2), wg_ref[...]))
        pp = _dot(_bf(p_ref[...]), wp_ref[...])
        o_ref[...] = h2 + gate * pp


def _mlp(h, w_up, w_down, g, b, p, w_gate, w_proj, tm=512, tf=1024):
    T, D = h.shape
    FF = w_up.shape[1]
    P = p.shape[1]
    return pl.pallas_call(
        _mlp_kernel,
        grid=(T // tm, FF // tf),
        in_specs=[pl.BlockSpec((tm, D), lambda i, k: (i, 0)),
                  pl.BlockSpec((D, tf), lambda i, k: (0, k)),
                  pl.BlockSpec((tf, D), lambda i, k: (k, 0)),
                  pl.BlockSpec((1, D), lambda i, k: (0, 0)),
                  pl.BlockSpec((1, D), lambda i, k: (0, 0)),
                  pl.BlockSpec((tm, P), lambda i, k: (i, 0)),
                  pl.BlockSpec((D, D), lambda i, k: (0, 0)),
                  pl.BlockSpec((P, D), lambda i, k: (0, 0))],
        out_specs=pl.BlockSpec((tm, D), lambda i, k: (i, 0)),
        out_shape=jax.ShapeDtypeStruct((T, D), F32),
        scratch_shapes=[pltpu.VMEM((tm, D), BF16), pltpu.VMEM((tm, D), F32)],
        compiler_params=_params("parallel", "arbitrary"),
        name="mlp",
    )(h, w_up, w_down, g.reshape(1, D), b.reshape(1, D), p, w_gate, w_proj)


def _bucket_tiles(t):
    i = np.arange(t)[:, None]
    j = np.arange(t)[None, :]
    out = []
    for d in range(2):
        rel = i - j + d * t
        n = np.maximum(rel, 0)
        max_exact = REL_BUCKETS // 2
        nf = np.maximum(n, 1).astype(np.float32)
        large = max_exact + (np.log(nf / np.float32(max_exact))
                             / np.float32(math.log(REL_MAX_DIST / max_exact))
                             * np.float32(REL_BUCKETS - max_exact)).astype(np.int32)
        large = np.minimum(large, REL_BUCKETS - 1)
        bucket = np.where(n < max_exact, n, large)
        out.append(np.where(rel >= 0, bucket, -1))
    return np.stack(out).astype(np.int32)


def _bias_tiles_kernel(rb_ref, bucket_ref, o_ref):
    h = pl.program_id(0)
    bucket = bucket_ref[...]
    far = rb_ref[REL_BUCKETS - 1, h]
    acc = jnp.zeros(bucket.shape, F32)
    for b in range(REL_BUCKETS):
        acc = jnp.where(bucket == b, rb_ref[b, h] - far, acc)
    o_ref[0] = jnp.where(bucket < 0, MASK_VALUE, acc)


def _bias_tiles(rel_bias, t):
    H = rel_bias.shape[1]
    buckets = jnp.asarray(_bucket_tiles(t))
    return pl.pallas_call(
        _bias_tiles_kernel,
        grid=(H,),
        in_specs=[pl.BlockSpec(memory_space=pltpu.SMEM),
                  pl.BlockSpec((2, t, t), lambda h: (0, 0, 0))],
        out_specs=pl.BlockSpec((1, 2, t, t), lambda h: (h, 0, 0, 0)),
        out_shape=jax.ShapeDtypeStruct((H, 2, t, t), F32),
        compiler_params=_params("parallel"),
        name="bias_tiles",
    )(rel_bias, buckets)


def _attn_kernel(q_ref, k_ref, v_ref, bias_ref, lam_ref, sg_ref, o_ref, *, t, lam_init):
    qi = pl.program_id(2)
    d = DA_HEAD_DIM
    q = q_ref[0] * jnp.asarray(d ** -0.5, BF16)
    lane = lax.broadcasted_iota(jnp.int32, q.shape, 1)
    zero = jnp.zeros_like(q)
    qs = (jnp.where(lane < d, q, zero), jnp.where(lane >= d, q, zero))

    def block_rows(j):
        return pl.ds(pl.multiple_of(j * t, t), t)

    def logits(j):
        k = k_ref[0, block_rows(j), :]
        return tuple(_dot_nt(qs[mi], k) for mi in range(2))

    def accumulate(j, s, state, tile):
        v = v_ref[0, block_rows(j), :]
        new = []
        for mi in range(2):
            m_old, l_old, acc_old = state[mi]
            s_m = s[mi] if tile is None else s[mi] + bias_ref[0, tile]
            m_new = jnp.maximum(m_old, jnp.max(s_m, -1, keepdims=True))
            p = jnp.exp(s_m - m_new)
            alpha = jnp.exp(m_old - m_new)
            l_new = alpha * l_old + jnp.sum(p, -1, keepdims=True)
            acc_new = alpha * acc_old + _dot(_bf(p), v)
            new.append((m_new, l_new, acc_new))
        return tuple(new)

    def pipelined(j, carry, tile):
        s, state = carry
        s_next = logits(j + 1)
        return s_next, accumulate(j, s, state, tile)

    init_map = (jnp.full((t, 1), MASK_VALUE, F32), jnp.zeros((t, 1), F32),
                jnp.zeros((t, DA_V_DIM), F32))
    carry = (logits(0), (init_map, init_map))
    carry = lax.fori_loop(0, jnp.maximum(qi - 1, 0), lambda j, c: pipelined(j, c, None), carry)
    carry = lax.cond(qi > 0, lambda c: pipelined(qi - 1, c, 1), lambda c: c, carry)
    s_last, state = carry
    state = accumulate(qi, s_last, state, 0)

    lv = lam_ref[...]
    lam = (jnp.exp(jnp.sum(lv[0:1] * lv[1:2], -1, keepdims=True))
           - jnp.exp(jnp.sum(lv[2:3] * lv[3:4], -1, keepdims=True)) + lam_init)
    (_, l0, acc0), (_, l1, acc1) = state
    o = acc0 / l0 - lam * (acc1 / l1)
    o = o * lax.rsqrt(jnp.mean(o * o, -1, keepdims=True) + DA_EPS) * sg_ref[...]
    o_ref[0] = (o * (1.0 - lam_init)).astype(o_ref.dtype)


def _diff_attention_core(qkv, bias_tiles, lam_vecs, subln_g, lam_init, t):
    B, S, _ = qkv.shape
    H = DA_HEADS
    W = 2 * DA_HEAD_DIM
    return pl.pallas_call(
        functools.partial(_attn_kernel, t=t, lam_init=lam_init),
        grid=(B, H, S // t),
        in_specs=[pl.BlockSpec((1, t, W), lambda b, h, i: (b, i, h)),
                  pl.BlockSpec((1, S, W), lambda b, h, i: (b, 0, H + h)),
                  pl.BlockSpec((1, S, W), lambda b, h, i: (b, 0, 2 * H + h)),
                  pl.BlockSpec((1, 2, t, t), lambda b, h, i: (h, 0, 0, 0)),
                  pl.BlockSpec((4, DA_HEAD_DIM), lambda b, h, i: (0, 0)),
                  pl.BlockSpec((1, DA_V_DIM), lambda b, h, i: (0, 0))],
        out_specs=pl.BlockSpec((1, t, W), lambda b, h, i: (b, i, h)),
        out_shape=jax.ShapeDtypeStruct((B, S, D_MODEL), BF16),
        compiler_params=_params("parallel", "parallel", "arbitrary"),
        name="diff_attn",
    )(qkv, qkv, qkv, bias_tiles, lam_vecs, subln_g.reshape(1, DA_V_DIM))


def _gla_kernel(q_ref, k_ref, v_ref, r_ref, za_ref, wa2_ref, ba_ref, ng_ref, o_ref, s_ref, *, L):
    C = GLA_CHUNK

    @pl.when(pl.program_id(2) == 0)
    def _():
        s_ref[...] = jnp.zeros_like(s_ref)

    row = lax.broadcasted_iota(jnp.int32, (C, C), 0)
    col = lax.broadcasted_iota(jnp.int32, (C, C), 1)
    tril = row >= col
    tril_f = tril.astype(F32)
    ones = jnp.ones((C, GLA_DV), F32)

    def chunk(c, _):
        rows = pl.ds(pl.multiple_of(c * C, C), C)
        z = _dot(za_ref[0, rows, :], wa2_ref[...], HI) + ba_ref[...]
        log_a = -_softplus(-z) * (1.0 / GLA_TAU)
        b = _dot(tril_f, log_a, HI)
        q = q_ref[0, rows, :].astype(F32) * (GLA_DK ** -0.5)
        k = k_ref[0, rows, :].astype(F32)
        v = v_ref[0, rows, :]
        q_dec = _bf(q * jnp.exp(b))
        att = _dot_nt(q_dec, _bf(k * jnp.exp(-b)))
        att = jnp.where(tril, att, 0.0)
        state = s_ref[...]
        o = _dot(_bf(att), v) + _dot(q_dec, _bf(state))
        b_last = b[C - 1:C, :]
        k_to_end = _bf(k * jnp.exp(b_last - b))
        decay = jnp.exp(_dot_tn(log_a, ones, HI))
        s_ref[...] = state * decay + _dot_tn(k_to_end, v)
        o = o * lax.rsqrt(jnp.mean(o * o, -1, keepdims=True) + GLA_EPS) * ng_ref[...]
        r = r_ref[0, rows, :].astype(F32)
        o_ref[0, rows, :] = (o * (r * _sigmoid(r))).astype(o_ref.dtype)
        return 0

    lax.fori_loop(0, L // C, chunk, 0)


def _gla_core(qkvr, za, w_a2p, b_a, norm_g, L=512):
    B, S, _ = qkvr.shape
    H, dk, dv = GLA_HEADS, GLA_DK, GLA_DV
    return pl.pallas_call(
        functools.partial(_gla_kernel, L=L),
        grid=(B, H, S // L),
        in_specs=[pl.BlockSpec((1, L, dk), lambda b, h, l: (b, l, h)),
                  pl.BlockSpec((1, L, dk), lambda b, h, l: (b, l, H + h)),
                  pl.BlockSpec((1, L, dv), lambda b, h, l: (b, l, H + h)),
                  pl.BlockSpec((1, L, dv), lambda b, h, l: (b, l, 2 * H + h)),
                  pl.BlockSpec((1, L, LANES), lambda b, h, l: (b, l, 0)),
                  pl.BlockSpec((LANES, dk), lambda b, h, l: (0, h)),
                  pl.BlockSpec((1, dk), lambda b, h, l: (0, h)),
                  pl.BlockSpec((1, dv), lambda b, h, l: (0, 0))],
        out_specs=pl.BlockSpec((1, L, dv), lambda b, h, l: (b, l, h)),
        out_shape=jax.ShapeDtypeStruct((B, S, D_MODEL), BF16),
        scratch_shapes=[pltpu.VMEM((dk, dv), F32)],
        compiler_params=_params("parallel", "parallel", "arbitrary"),
        name="gla",
    )(qkvr, qkvr, qkvr, qkvr, za, w_a2p, b_a.reshape(1, H * dk), norm_g.reshape(1, dv))


def _rw_proj_kernel(x_ref, xp_ref, mix_ref, wrkv_ref, w0_ref, w1_ref, w2_ref, a0_ref, a1_ref,
                    a2_ref, g1_ref, g2_ref, r_o, k_o, v_o, lw_o, a_o, g_o, *, tm, S):
    i = pl.program_id(0)
    x = x_ref[...]
    at_start = (i * tm) % S == 0
    prev = jnp.where(at_start, 0.0, xp_ref[7:8, :])
    row = lax.broadcasted_iota(jnp.int32, x.shape, 0)
    x_prev = jnp.where(row == 0, prev, pltpu.roll(x, 1, 0))
    xx = x_prev - x

    def mixed(gi):
        return _bf(x + xx * mix_ref[gi:gi + 1, :])

    r_o[...] = _dot(mixed(0), wrkv_ref[0])
    k_o[...] = _dot(mixed(1), wrkv_ref[1])
    v_o[...] = _dot(mixed(2), wrkv_ref[2])
    u = w0_ref[...] + _dot(_bf(jnp.tanh(_dot(mixed(3), w1_ref[...]))), w2_ref[...])
    lw_o[...] = -jnp.exp(-_softplus(-u) - 0.5)
    a_o[...] = _sigmoid(a0_ref[...] + _dot(_bf(_dot(mixed(4), a1_ref[...])), a2_ref[...]))
    g_o[...] = _dot(_bf(_sigmoid(_dot(mixed(5), g1_ref[...]))), g2_ref[...])


def _rw_proj(h, S, mix, w_rkv, w0, w1, w2, a0, a1, a2, g1, g2, tm=256):
    T, D = h.shape
    full = lambda a: pl.BlockSpec(a.shape, lambda i: (0,) * a.ndim)
    w0, a0 = w0.reshape(1, D), a0.reshape(1, D)
    out = jax.ShapeDtypeStruct((T, D), F32)
    tile = pl.BlockSpec((tm, D), lambda i: (i, 0))
    return pl.pallas_call(
        functools.partial(_rw_proj_kernel, tm=tm, S=S),
        grid=(T // tm,),
        in_specs=[tile,
                  pl.BlockSpec((8, D), lambda i: (jnp.maximum(i * (tm // 8) - 1, 0), 0)),
                  full(mix), full(w_rkv), full(w0), full(w1), full(w2), full(a0), full(a1),
                  full(a2), full(g1), full(g2)],
        out_specs=[tile] * 6,
        out_shape=[out] * 6,
        compiler_params=_params("parallel"),
        name="rw_proj",
    )(h, h, mix, w_rkv, w0, w1, w2, a0, a1, a2, g1, g2)


def _cumsum_rows(x):
    n = x.shape[0]
    row = lax.broadcasted_iota(jnp.int32, x.shape, 0)
    shift = 1
    while shift < n:
        x = x + jnp.where(row >= shift, pltpu.roll(x, shift, 0), 0.0)
        shift *= 2
    return x


def _rw_scan_kernel(r_ref, k_ref, v_ref, a_ref, lw_ref, g_ref, kk_ref, ka_ref, rk_ref, lng_ref,
                    lnb_ref, o_ref, h_ref, *, L, G):
    C = RW_CHUNK
    N = RW_HEAD

    @pl.when(pl.program_id(2) == 0)
    def _():
        h_ref[...] = jnp.zeros_like(h_ref)

    row = lax.broadcasted_iota(jnp.int32, (C, C), 0)
    col = lax.broadcasted_iota(jnp.int32, (C, C), 1)
    incl = row >= col
    strict = row > col
    eye = row == col
    eye_f = eye.astype(F32)

    def chunk(c, _):
        rows = pl.ds(pl.multiple_of(c * C, C), C)
        rs, ks, vs, as_, lws, gs = (ref[0, rows, :] for ref in (r_ref, k_ref, v_ref, a_ref, lw_ref, g_ref))
        heads = range(G)
        cols = [slice(hd * N, (hd + 1) * N) for hd in heads]
        split = lambda slab: [slab[:, ls] for ls in cols]
        gcs = _cumsum_rows(lws)
        g_last = gcs[C - 1:C, :]
        r, v, a = split(rs), split(vs), split(as_)
        kk = split(ks * kk_ref[...])
        kk = [x * lax.rsqrt(jnp.maximum(jnp.sum(x * x, -1, keepdims=True), 1e-24)) for x in kk]
        bb = [x * y for x, y in zip(kk, a)]
        k = split(ks * (1.0 + (as_ - 1.0) * ka_ref[...]))
        inv = split(jnp.exp(-gcs))
        to_end = split(jnp.exp(g_last - gcs))
        excl = split(jnp.exp(gcs - lws))
        r_g = [x * y for x, y in zip(r, split(jnp.exp(gcs)))]
        end_decay = split(jnp.exp(g_last))
        v_b = [_bf(x) for x in v]
        ar = [_bf(jnp.concatenate([-x * e, y], 0)) for x, e, y in zip(kk, excl, r_g)]
        b_t = [_bf(x * y) for x, y in zip(bb, inv)]
        k_t = [_bf(x * y) for x, y in zip(k, inv)]
        b_d = [_bf(x * y) for x, y in zip(bb, to_end)]
        k_d = [_bf(x * y) for x, y in zip(k, to_end)]
        pb = [_dot_nt(x, y) for x, y in zip(ar, b_t)]
        pk = [_dot_nt(x, y) for x, y in zip(ar, k_t)]
        a_ab = [jnp.where(strict, x[:C], 0.0) for x in pb]
        a_rb = [_bf(jnp.where(incl, x[C:], 0.0)) for x in pb]
        a_ak = [_bf(jnp.where(strict, x[:C], 0.0)) for x in pk]
        a_rk = [_bf(jnp.where(incl, x[C:], 0.0)) for x in pk]
        akv = [_bf(_dot(x, y)) for x, y in zip(a_ak, v_b)]
        rkv = [_dot(x, y) for x, y in zip(a_rk, v_b)]
        kdv = [_dot_tn(x, y) for x, y in zip(k_d, v_b)]
        t_inv = [eye_f + x for x in a_ab]
        pw = [_bf(x) for x in a_ab]
        for step in range(int(math.log2(C)) - 1):
            pw_next = [_bf(_dot(x, x)) for x in pw]
            if step > 0:
                t_inv = [t + _dot(_bf(t), x) for t, x in zip(t_inv, pw)]
            pw = pw_next
        t_b = [_bf(t + _dot(_bf(t), x)) for t, x in zip(t_inv, pw)]
        w1 = [_bf(_dot(t, x[:C])) for t, x in zip(t_b, ar)]
        w2 = [_bf(_dot(t, x)) for t, x in zip(t_b, akv)]
        q_rw1 = [_dot(x, y) for x, y in zip(a_rb, w1)]
        q_rw2 = [_dot(x, y) for x, y in zip(a_rb, w2)]
        q_bw1 = [_dot_tn(x, y) for x, y in zip(b_d, w1)]
        q_bw2 = [_dot_tn(x, y) for x, y in zip(b_d, w2)]
        rm = [_bf(jnp.concatenate([x + y, jnp.where(eye, e, 0.0) + z], 0))
              for x, y, e, z in zip(r_g, q_rw1, end_decay, q_bw1)]
        h0 = [_bf(h_ref[hd]) for hd in heads]
        yh = [_dot(x, y) for x, y in zip(rm, h0)]
        for hd in heads:
            h_ref[hd] = yh[hd][C:] + q_bw2[hd] + kdv[hd]
        outs = []
        for hd in heads:
            ls = cols[hd]
            y = yh[hd][:C] + q_rw2[hd] + rkv[hd]
            y = _layer_norm(y, lng_ref[:, ls], lnb_ref[:, ls], RW_GN_EPS)
            y = y + jnp.sum(r[hd] * k[hd] * rk_ref[:, ls], -1, keepdims=True) * v[hd]
            outs.append(y)
        o_ref[0, rows, :] = (jnp.concatenate(outs, -1) * gs).astype(o_ref.dtype)
        return 0

    lax.fori_loop(0, L // C, chunk, 0)


def _rw_scan(r, k, v, a, lw, g, k_k, k_a, r_k, ln_g, ln_b, L=256, G=16):
    B, S, D = r.shape
    W = G * RW_HEAD
    seq = pl.BlockSpec((1, L, W), lambda b, h, l: (b, l, h))
    par = pl.BlockSpec((1, W), lambda b, h, l: (0, h))
    vec = lambda p: p.reshape(1, D)
    return pl.pallas_call(
        functools.partial(_rw_scan_kernel, L=L, G=G),
        grid=(B, D // W, S // L),
        in_specs=[seq] * 6 + [par] * 5,
        out_specs=seq,
        out_shape=jax.ShapeDtypeStruct((B, S, D), BF16),
        scratch_shapes=[pltpu.VMEM((G, RW_HEAD, RW_HEAD), F32)],
        compiler_params=_params("parallel", "parallel", "arbitrary"),
        name="rw_scan",
    )(r, k, v, a, lw, g, vec(k_k), vec(k_a), vec(r_k), vec(ln_g), vec(ln_b))


def kernel(x, p, rel_bias, ln1_g, ln1_b, ln2_g, ln2_b, mlp_up, mlp_down, ple_proj, ple_gate, da_w_qkv, da_w_o, da_lam_q1, da_lam_k1, da_lam_q2, da_lam_k2, da_subln_g, gla_w_in, gla_w_a1, gla_w_a2, gla_b_a, gla_norm_g, gla_w_o, rw_mix, rw_w_rkv, rw_w0, rw_w1, rw_w2, rw_a0, rw_a1, rw_a2, rw_g1, rw_g2, rw_k_k, rw_k_a, rw_r_k, rw_ln_g, rw_ln_b, rw_w_o):
    B, S, D = x.shape
    T = B * S
    attn_tile = 512
    h = x.reshape(T, D)
    bias_tiles = _bias_tiles(rel_bias, attn_tile)
    for i in range(DEPTH):
        kind, j = i % N_MIXERS, i // N_MIXERS
        if kind == 0:
            lam_init = 0.8 - 0.6 * math.exp(-0.3 * i)
            qkv = _proj(h, _bf(da_w_qkv[j]), BF16).reshape(B, S, 3 * D)
            lam_vecs = jnp.stack([da_lam_q1[j], da_lam_k1[j], da_lam_q2[j], da_lam_k2[j]])
            mixed = _diff_attention_core(qkv, bias_tiles, lam_vecs, da_subln_g[j], lam_init, attn_tile)
            w_o = da_w_o[j]
        elif kind == 1:
            qkvr = _proj(h, _bf(gla_w_in[j]), BF16).reshape(B, S, 3 * D)
            w_a1p = jnp.pad(gla_w_a1[j], ((0, 0), (0, LANES - GLA_GATE_RANK)))
            w_a2p = jnp.pad(gla_w_a2[j], ((0, LANES - GLA_GATE_RANK), (0, 0)))
            za = _proj(h, _bf(w_a1p), F32).reshape(B, S, LANES)
            mixed = _gla_core(qkvr, za, w_a2p, gla_b_a[j], gla_norm_g[j])
            w_o = gla_w_o[j]
        else:
            r, k, v, lw, a, g = _rw_proj(
                h, S, rw_mix[j], _bf(rw_w_rkv[j]), rw_w0[j], _bf(rw_w1[j]), _bf(rw_w2[j]), rw_a0[j],
                _bf(rw_a1[j]), _bf(rw_a2[j]), _bf(rw_g1[j]), _bf(rw_g2[j]))
            sh = lambda t: t.reshape(B, S, D)
            mixed = _rw_scan(sh(r), sh(k), sh(v), sh(a), sh(lw), sh(g), rw_k_k[j], rw_k_a[j],
                             rw_r_k[j].reshape(D), rw_ln_g[j], rw_ln_b[j])
            w_o = rw_w_o[j]
        h = _proj_ln(mixed.reshape(T, D), _bf(w_o), h, ln1_g[i], ln1_b[i])
        h = _mlp(h, _bf(mlp_up[i]), _bf(mlp_down[i]), ln2_g[i], ln2_b[i], p[i].reshape(T, PLE_DIM),
                 _bf(ple_gate[i]), _bf(ple_proj[i]))
    return h.reshape(B, S, D)
```
